```python
import math
import jax, jax.numpy as jnp
from jax import lax
import numpy as np

D_MODEL = 1024
BATCH = 8
SEQ = 2048
DEPTH = 4
DEC_BATCH = 128
DEC_SEQ = 1
PAST_LEN = 2048
PAGE_SIZE = 128

ATT_GROUPS = ((128, 1), (512, 4), (2048, 16))
N_GROUPS = 3
HEADS_PER_GROUP = 4
HEAD_DIM = 128
N_ATT_HEADS = N_GROUPS * HEADS_PER_GROUP
ATT_QKV = N_ATT_HEADS * HEAD_DIM
ATT_OUT = HEADS_PER_GROUP * HEAD_DIM
CONV_WIDTH = 512
CONV_K = 31
SGU_WIDTH = 512
SGU_GROUPS = 4
SGU_GROUP_DIM = SGU_WIDTH // SGU_GROUPS
CHUNK = 128
N_BRANCH = 3
NEG_INF = -1e30
RMS_EPS = 1e-6
LN_EPS = 1e-5

SPLIT_SIZES = (ATT_QKV, ATT_QKV, ATT_QKV, ATT_OUT,
               2 * CONV_WIDTH, CONV_WIDTH,
               SGU_WIDTH, SGU_WIDTH, SGU_WIDTH,
               N_BRANCH * D_MODEL)
N_IN_COLS = sum(SPLIT_SIZES)

kernel_name = "hybrid_dilated_conv_sgu_decoder_step"


def rms_norm(x, g):
    xf = x.astype(jnp.float32)
    y = xf * lax.rsqrt(jnp.mean(xf * xf, axis=-1, keepdims=True) + RMS_EPS)
    return (y * g.astype(jnp.float32)).astype(x.dtype)


def layer_norm(x, g, b):
    xf = x.astype(jnp.float32)
    mu = jnp.mean(xf, axis=-1, keepdims=True)
    var = jnp.mean(jnp.square(xf - mu), axis=-1, keepdims=True)
    y = (xf - mu) * lax.rsqrt(var + LN_EPS)
    return (y * g.astype(jnp.float32) + b.astype(jnp.float32)).astype(x.dtype)


def alibi_slopes():
    s = [2.0 ** (-8.0 * (h + 1) / N_ATT_HEADS) for h in range(N_ATT_HEADS)]
    return jnp.asarray(s, dtype=jnp.float32).reshape(N_GROUPS, HEADS_PER_GROUP)


def project_inputs(x, norm_g, w_in):
    h = rms_norm(x, norm_g)
    z = h @ w_in
    points = [int(p) for p in np.cumsum(SPLIT_SIZES)[:-1]]
    return jnp.split(z, points, axis=-1)


def dilated_group_prompt(q, k, v, slopes, window, dil):
    B, S, Hg, E = q.shape
    K = window // dil
    L = S // dil
    nb = -(-L // K)
    Lp = nb * K
    scale = 1.0 / math.sqrt(E)

    def strided(t):
        t = t.reshape(B, L, dil, Hg, E).transpose(0, 2, 1, 3, 4)
        t = jnp.pad(t, ((0, 0), (0, 0), (0, Lp - L), (0, 0), (0, 0)))
        return t.reshape(B, dil, nb, K, Hg, E)

    def with_prev(t):
        prev = jnp.pad(t, ((0, 0), (0, 0), (1, 0), (0, 0), (0, 0), (0, 0)))[:, :, :-1]
        return jnp.concatenate([prev, t], axis=3)

    qb = strided(q)
    kk = with_prev(strided(k))
    vv = with_prev(strided(v))
    i = jnp.arange(K)[:, None]
    j = jnp.arange(2 * K)[None, :]
    delta = K + i - j
    band = (delta >= 0) & (delta <= K)
    valid = band[None] & ((jnp.arange(nb)[:, None, None] > 0) | (j[None] >= K))
    bias = -slopes[:, None, None] * (delta * dil).astype(jnp.float32)[None]
    s = jnp.einsum('brnihe,brnjhe->brnhij', qb, kk, preferred_element_type=jnp.float32) * scale
    s = jnp.where(valid[None, None, :, None], s + bias, NEG_INF)
    lse = jax.nn.logsumexp(s, axis=-1)
    p = jnp.exp(s - lse[..., None])
    o = jnp.einsum('brnhij,brnjhe->brnihe', p.astype(v.dtype), vv, preferred_element_type=jnp.float32)
    o = o.reshape(B, dil, Lp, Hg, E)[:, :, :L].transpose(0, 2, 1, 3, 4).reshape(B, S, Hg, E)
    lse = lse.transpose(0, 1, 2, 4, 3).reshape(B, dil, Lp, Hg)[:, :, :L]
    lse = lse.transpose(0, 2, 1, 3).reshape(B, S, Hg)
    return o, lse


def dilated_group_sample(q, k_new, v_new, k_buf, v_buf, slopes, window, dil):
    Wb = k_buf.shape[1]
    T = q.shape[1]
    E = q.shape[-1]
    K = window // dil
    scale = 1.0 / math.sqrt(E)
    ke = jnp.concatenate([k_buf, k_new], axis=1)
    ve = jnp.concatenate([v_buf, v_new], axis=1)
    m = jnp.arange(K + 1)
    idx = Wb + jnp.arange(T)[:, None] - m[None, :] * dil
    valid = idx >= 0
    idx = jnp.maximum(idx, 0)
    kg = ke[:, idx]
    vg = ve[:, idx]
    bias = -slopes[:, None] * (m * dil).astype(jnp.float32)[None, :]
    s = jnp.einsum('bthe,btmhe->bthm', q, kg, preferred_element_type=jnp.float32) * scale
    s = jnp.where(valid[None, :, None, :], s + bias, NEG_INF)
    lse = jax.nn.logsumexp(s, axis=-1)
    p = jnp.exp(s - lse[..., None])
    o = jnp.einsum('bthm,btmhe->bthe', p.astype(v_new.dtype), vg, preferred_element_type=jnp.float32)
    return o, lse


def combine_groups(outs, lses):
    o = jnp.stack(outs, axis=0)
    w = jax.nn.softmax(jnp.stack(lses, axis=0), axis=0)
    return jnp.sum(w[..., None] * o, axis=0)


def split_heads(t):
    return t.reshape(t.shape[:2] + (N_GROUPS, HEADS_PER_GROUP, HEAD_DIM))


def attention_prompt(q, k, v, slopes):
    B, S = q.shape[:2]
    q, k, v = split_heads(q), split_heads(k), split_heads(v)
    outs, lses, k_rows, v_rows = [], [], [], []
    for g, (window, dil) in enumerate(ATT_GROUPS):
        o, lse = dilated_group_prompt(q[:, :, g], k[:, :, g], v[:, :, g], slopes[g], window, dil)
        outs.append(o)
        lses.append(lse)
        keep = min(window, S)
        k_rows.append(k[:, S - keep:, g])
        v_rows.append(v[:, S - keep:, g])
    att = combine_groups(outs, lses).reshape(B, S, ATT_OUT).astype(q.dtype)
    return att, k_rows, v_rows


def attention_sample(q, k, v, k_bufs, v_bufs, slopes):
    Bd, T = q.shape[:2]
    q, k, v = split_heads(q), split_heads(k), split_heads(v)
    outs, lses, k_rows, v_rows = [], [], [], []
    for g, (window, dil) in enumerate(ATT_GROUPS):
        o, lse = dilated_group_sample(q[:, :, g], k[:, :, g], v[:, :, g], k_bufs[g], v_bufs[g],
                                      slopes[g], window, dil)
        outs.append(o)
        lses.append(lse)
        k_rows.append(k[:, :, g])
        v_rows.append(v[:, :, g])
    att = combine_groups(outs, lses).reshape(Bd, T, ATT_OUT).astype(q.dtype)
    return att, k_rows, v_rows


def conv_module(z, left, conv_w, conv_b, ln_g, ln_b):
    a, gt = jnp.split(z, 2, axis=-1)
    a = a * jax.nn.sigmoid(gt)
    xp = jnp.concatenate([left, a], axis=1)
    y = lax.conv_general_dilated(xp, conv_w[:, None, :], (1,), 'VALID',
                                 dimension_numbers=('NWC', 'WIO', 'NWC'),
                                 feature_group_count=CONV_WIDTH) + conv_b
    y = jax.nn.silu(layer_norm(y, ln_g, ln_b))
    return y, xp[:, -(CONV_K - 1):]


def spatial_gating(u, v, ln_g, ln_b, w_s, b_s):
    Bn, T, C = v.shape
    vn = layer_norm(v, ln_g, ln_b)
    nc = -(-T // CHUNK)
    Tp = nc * CHUNK
    vc = jnp.pad(vn, ((0, 0), (0, Tp - T), (0, 0))).reshape(Bn, nc, CHUNK, SGU_GROUPS, SGU_GROUP_DIM)
    wm = jnp.where(jnp.tril(jnp.ones((CHUNK, CHUNK), dtype=bool)), w_s, 0.0)
    s = jnp.einsum('gij,bcjge->bcige', wm, vc) + b_s.T[:, :, None]
    s = s.reshape(Bn, Tp, C)[:, :T]
    return u * s, vn


def branch_merge(x, att, gate_a, conv_y, gate_b, sgu_y, gate_c, gate_m, w_pa, w_pb, w_pc, w_out):
    pa = (att * jax.nn.silu(gate_a)) @ w_pa
    pb = (conv_y * jax.nn.silu(gate_b)) @ w_pb
    pc = (sgu_y * jax.nn.silu(gate_c)) @ w_pc
    gm = jax.nn.sigmoid(gate_m).reshape(gate_m.shape[:-1] + (N_BRANCH, D_MODEL))
    merged = gm[..., 0, :] * pa + gm[..., 1, :] * pb + gm[..., 2, :] * pc
    return x + merged @ w_out


def setup_inputs(seed: int = 0) -> dict:
    key = jax.random.key(seed)
    ks = jax.random.split(key, 32)
    f32 = jnp.float32

    def nrm(k, shape, scale):
        return jax.random.normal(k, shape, f32) * scale

    wb = [min(w, PAST_LEN) for w, _ in ATT_GROUPS]
    inputs = {}
    inputs["x_prompt"] = nrm(ks[0], (BATCH, SEQ, D_MODEL), 1.0)
    inputs["x_sample"] = nrm(ks[1], (DEC_BATCH, DEC_SEQ, D_MODEL), 1.0)
    for g, (w, _) in enumerate(ATT_GROUPS):
        shape = (DEPTH, DEC_BATCH, wb[g], HEADS_PER_GROUP, HEAD_DIM)
        inputs["cache_k_w%d" % w] = nrm(ks[2 + 2 * g], shape, 1.0)
        inputs["cache_v_w%d" % w] = nrm(ks[3 + 2 * g], shape, 1.0)
    inputs["state_conv"] = nrm(ks[8], (DEPTH, DEC_BATCH, CONV_K - 1, CONV_WIDTH), 0.5)
    inputs["norm_g"] = 1.0 + nrm(ks[9], (DEPTH, D_MODEL), 0.02)
    inputs["w_in"] = nrm(ks[10], (DEPTH, D_MODEL, N_IN_COLS), D_MODEL ** -0.5)
    inputs["conv_w"] = nrm(ks[11], (DEPTH, CONV_K, CONV_WIDTH), CONV_K ** -0.5)
    inputs["conv_b"] = nrm(ks[12], (DEPTH, CONV_WIDTH), 0.02)
    inputs["conv_ln_g"] = 1.0 + nrm(ks[13], (DEPTH, CONV_WIDTH), 0.02)
    inputs["conv_ln_b"] = nrm(ks[14], (DEPTH, CONV_WIDTH), 0.02)
    inputs["sgu_ln_g"] = 1.0 + nrm(ks[15], (DEPTH, SGU_WIDTH), 0.02)
    inputs["sgu_ln_b"] = nrm(ks[16], (DEPTH, SGU_WIDTH), 0.02)
    inputs["sgu_w"] = nrm(ks[17], (DEPTH, SGU_GROUPS, CHUNK, CHUNK), CHUNK ** -0.5)
    inputs["sgu_b"] = 1.0 + nrm(ks[18], (DEPTH, SGU_GROUPS, CHUNK), 0.02)
    inputs["w_pa"] = nrm(ks[19], (DEPTH, ATT_OUT, D_MODEL), ATT_OUT ** -0.5)
    inputs["w_pb"] = nrm(ks[20], (DEPTH, CONV_WIDTH, D_MODEL), CONV_WIDTH ** -0.5)
    inputs["w_pc"] = nrm(ks[21], (DEPTH, SGU_WIDTH, D_MODEL), SGU_WIDTH ** -0.5)
    inputs["w_out"] = nrm(ks[22], (DEPTH, D_MODEL, D_MODEL), D_MODEL ** -0.5)
    inputs["final_g"] = 1.0 + nrm(ks[23], (D_MODEL,), 0.02)
    return inputs


def reference(x_prompt, x_sample, cache_k_w128, cache_v_w128, cache_k_w512, cache_v_w512,
              cache_k_w2048, cache_v_w2048, state_conv, norm_g, w_in, conv_w, conv_b,
              conv_ln_g, conv_ln_b, sgu_ln_g, sgu_ln_b, sgu_w, sgu_b, w_pa, w_pb, w_pc, w_out, final_g):
    slopes = alibi_slopes()
    k_caches = (cache_k_w128, cache_k_w512, cache_k_w2048)
    v_caches = (cache_v_w128, cache_v_w512, cache_v_w2048)
    xp, xs = x_prompt, x_sample
    kp_rows = [[] for _ in ATT_GROUPS]
    vp_rows = [[] for _ in ATT_GROUPS]
    ks_rows = [[] for _ in ATT_GROUPS]
    vs_rows = [[] for _ in ATT_GROUPS]
    conv_p, conv_s, sgu_v_s = [], [], []
    for l in range(DEPTH):
        q, k, v, ga, zb, gb, u, vc, gc, gm = project_inputs(xp, norm_g[l], w_in[l])
        att, kr, vr = attention_prompt(q, k, v, slopes)
        left = jnp.zeros((xp.shape[0], CONV_K - 1, CONV_WIDTH), xp.dtype)
        cy, cbuf = conv_module(zb, left, conv_w[l], conv_b[l], conv_ln_g[l], conv_ln_b[l])
        sy, _ = spatial_gating(u, vc, sgu_ln_g[l], sgu_ln_b[l], sgu_w[l], sgu_b[l])
        xp = branch_merge(xp, att, ga, cy, gb, sy, gc, gm, w_pa[l], w_pb[l], w_pc[l], w_out[l])
        for g in range(N_GROUPS):
            kp_rows[g].append(kr[g])
            vp_rows[g].append(vr[g])
        conv_p.append(cbuf)
        q, k, v, ga, zb, gb, u, vc, gc, gm = project_inputs(xs, norm_g[l], w_in[l])
        att, kr, vr = attention_sample(q, k, v, [c[l] for c in k_caches], [c[l] for c in v_caches], slopes)
        cy, cbuf = conv_module(zb, state_conv[l], conv_w[l], conv_b[l], conv_ln_g[l], conv_ln_b[l])
        sy, vn = spatial_gating(u, vc, sgu_ln_g[l], sgu_ln_b[l], sgu_w[l], sgu_b[l])
        xs = branch_merge(xs, att, ga, cy, gb, sy, gc, gm, w_pa[l], w_pb[l], w_pc[l], w_out[l])
        for g in range(N_GROUPS):
            ks_rows[g].append(kr[g])
            vs_rows[g].append(vr[g])
        conv_s.append(cbuf)
        sgu_v_s.append(vn)
    y_prompt = rms_norm(xp, final_g)
    y_sample = rms_norm(xs, final_g)
    new_k_w128_prompt, new_k_w512_prompt, new_k_w2048_prompt = [jnp.stack(r) for r in kp_rows]
    new_v_w128_prompt, new_v_w512_prompt, new_v_w2048_prompt = [jnp.stack(r) for r in vp_rows]
    new_k_w128_sample, new_k_w512_sample, new_k_w2048_sample = [jnp.stack(r) for r in ks_rows]
    new_v_w128_sample, new_v_w512_sample, new_v_w2048_sample = [jnp.stack(r) for r in vs_rows]
    new_conv_prompt = jnp.stack(conv_p)
    new_conv_sample = jnp.stack(conv_s)
    new_sgu_v_sample = jnp.stack(sgu_v_s)
    return (y_prompt, y_sample,
            new_k_w128_prompt, new_v_w128_prompt, new_k_w512_prompt, new_v_w512_prompt,
            new_k_w2048_prompt, new_v_w2048_prompt, new_conv_prompt,
            new_k_w128_sample, new_v_w128_sample, new_k_w512_sample, new_v_w512_sample,
            new_k_w2048_sample, new_v_w2048_sample, new_conv_sample, new_sgu_v_sample)
```

```python
import functools
import math

import jax
import jax.numpy as jnp
from jax import lax
from jax.experimental import pallas as pl
from jax.experimental.pallas import tpu as pltpu

F32 = jnp.float32
BF16 = jnp.bfloat16

D_MODEL = 1024
DEPTH = 4
ATT_GROUPS = ((128, 1), (512, 4), (2048, 16))
N_GROUPS = 3
HEADS = 4
HEAD_DIM = 128
GW = HEADS * HEAD_DIM
ATT_QKV = N_GROUPS * GW
CONV_K = 31
CHUNK = 128
N_REST = 6656
C_GA, C_ZB, C_GB, C_U, C_VC, C_GC, C_GM = 0, 512, 1536, 2048, 2560, 3072, 3584
NEG_INF = -1e30
RMS_EPS = 1e-6
LN_EPS = 1e-5
SCALE = 1.0 / math.sqrt(HEAD_DIM)
SLOPES = [[2.0 ** (-8.0 * (g * HEADS + h + 1) / (N_GROUPS * HEADS)) for h in range(HEADS)]
          for g in range(N_GROUPS)]

TM = 512
VMEM_LIMIT = 56 * 1024 * 1024


def _rms(x, g):
    ms = jnp.mean(x * x, axis=-1, keepdims=True)
    return x * lax.rsqrt(ms + RMS_EPS) * g


def _ln(x, g, b):
    mu = jnp.mean(x, axis=-1, keepdims=True)
    xc = x - mu
    var = jnp.mean(xc * xc, axis=-1, keepdims=True)
    return xc * lax.rsqrt(var + LN_EPS) * g + b


def _sigmoid(x):
    return 1.0 / (1.0 + jnp.exp(-x))


def _silu(x):
    return x * _sigmoid(x)


def _resident(shape):
    nd = len(shape)
    return pl.BlockSpec(shape, lambda *_: (0,) * nd, pipeline_mode=pl.Buffered(1))


def _qkv_body(x_ref, g_ref, w_ref, *out_refs):
    qkv_bf = out_refs[:9]
    k1o, v1o, k2o, v2o, k3o, v3o = out_refs[9:]
    j = pl.program_id(1)
    last = pl.num_programs(1) - 1
    h = _rms(x_ref[...], g_ref[...]).astype(BF16)
    tails = {3: (k1o, k2o, k3o), 6: (v1o, v2o, v3o)}
    for c in range(9):
        z = jnp.dot(h, w_ref[:, c * GW:(c + 1) * GW], preferred_element_type=F32)
        if c < 3:
            qkv_bf[c][0] = (z * SCALE).astype(BF16)
            continue
        qkv_bf[c][0] = z.astype(BF16)
        base = 3 if c < 6 else 6
        g = c - base
        o128, o512, o2048 = tails[base]
        if g == 2:
            o2048[0] = z
        elif g == 1:
            @pl.when(j == last)
            def _():
                o512[0] = z
        else:
            @pl.when(j == last)
            def _():
                o128[0] = z[TM - 128:, :]


def _qkv_call(x, norm_g, w_qkv, B, S):
    nj = S // TM
    row = pl.BlockSpec((1, TM, GW), lambda b, j: (b, j, 0))
    per_b = lambda n: pl.BlockSpec((1, n, GW), lambda b, j: (b, 0, 0))
    out_shape = [jax.ShapeDtypeStruct((B, S, GW), BF16)] * 9 + [
        jax.ShapeDtypeStruct((B, 128, GW), F32), jax.ShapeDtypeStruct((B, 128, GW), F32),
        jax.ShapeDtypeStruct((B, 512, GW), F32), jax.ShapeDtypeStruct((B, 512, GW), F32),
        jax.ShapeDtypeStruct((B, S, GW), F32), jax.ShapeDtypeStruct((B, S, GW), F32)]
    out_specs = [row] * 9 + [per_b(128), per_b(128), per_b(512), per_b(512), row, row]
    return pl.pallas_call(
        _qkv_body,
        grid=(B, nj),
        in_specs=[pl.BlockSpec((TM, D_MODEL), lambda b, j: (b * nj + j, 0)),
                  _resident((1, D_MODEL)),
                  _resident((D_MODEL, 3 * ATT_QKV))],
        out_specs=out_specs,
        out_shape=out_shape,
        compiler_params=pltpu.CompilerParams(
            dimension_semantics=("arbitrary", "arbitrary"), vmem_limit_bytes=VMEM_LIMIT),
        name="prompt_qkv",
    )(x, norm_g, w_qkv)


def _att_body(*refs):
    (q1, k1p, k1c, v1p, v1c, q2, k2p, k2c, v2p, v2c, q3, k3c, v3c,
     o1, o2, o3, l1, l2, l3, bias_ref) = refs
    b = pl.program_id(0)
    j = pl.program_id(1)

    @pl.when((b == 0) & (j == 0))
    def _():
        i = lax.broadcasted_iota(jnp.int32, (CHUNK, CHUNK), 0)
        jj = lax.broadcasted_iota(jnp.int32, (CHUNK, CHUNK), 1)
        for g, (_, dil) in enumerate(ATT_GROUPS):
            for h in range(HEADS):
                sl = SLOPES[g][h] * dil
                d_prev = (CHUNK + i - jj).astype(F32)
                bias_ref[g, h, 0] = jnp.where(i <= jj, -sl * d_prev, NEG_INF)
                d_cur = (i - jj).astype(F32)
                bias_ref[g, h, 1] = jnp.where(i >= jj, -sl * d_cur, NEG_INF)

    groups = ((q1, k1p, k1c, v1p, v1c, o1, l1, j),
              (q2, k2p, k2c, v2p, v2c, o2, l2, j % 4),
              (q3, None, k3c, None, v3c, o3, l3, None))
    nt = (((1,), (1,)), ((), ()))
    for g, (q_ref, kp_ref, kc_ref, vp_ref, vc_ref, o_ref, l_ref, n) in enumerate(groups):
        for h in range(HEADS):
            hs = slice(h * HEAD_DIM, (h + 1) * HEAD_DIM)
            q = q_ref[0, :, hs]
            sc = lax.dot_general(q, kc_ref[0, :, hs], nt, preferred_element_type=F32) + bias_ref[g, h, 1]
            m = jnp.max(sc, axis=-1, keepdims=True)
            if kp_ref is not None:
                pen = jnp.where(n == 0, NEG_INF, 0.0).astype(F32)
                sp = (lax.dot_general(q, kp_ref[0, :, hs], nt, preferred_element_type=F32)
                      + bias_ref[g, h, 0] + pen)
                m = jnp.maximum(m, jnp.max(sp, axis=-1, keepdims=True))
                pp = jnp.exp(sp - m)
                pc = jnp.exp(sc - m)
                l = jnp.sum(pp, axis=-1, keepdims=True) + jnp.sum(pc, axis=-1, keepdims=True)
                o = (jnp.dot(pp.astype(BF16), vp_ref[0, :, hs], preferred_element_type=F32)
                     + jnp.dot(pc.astype(BF16), vc_ref[0, :, hs], preferred_element_type=F32))
            else:
                pc = jnp.exp(sc - m)
                l = jnp.sum(pc, axis=-1, keepdims=True)
                o = jnp.dot(pc.astype(BF16), vc_ref[0, :, hs], preferred_element_type=F32)
            o_ref[0, :, hs] = (o / l).astype(BF16)
            l_ref[0, :, hs] = jnp.broadcast_to(m + jnp.log(l), (CHUNK, HEAD_DIM))


def _att_call(qkv, B, S):
    q1, q2, q3, k1, k2, k3, v1, v2, v3 = qkv
    views = []
    for arrs, (_, dil) in zip(((q1, k1, v1), (q2, k2, v2), (q3, k3, v3)), ATT_GROUPS):
        views.append([a.reshape(B, S // dil, dil * GW) for a in arrs])
    blk = (1, CHUNK, GW)
    c1 = pl.BlockSpec(blk, lambda b, j: (b, j, 0))
    p1 = pl.BlockSpec(blk, lambda b, j: (b, jnp.maximum(j - 1, 0), 0))
    c2 = pl.BlockSpec(blk, lambda b, j: (b, j % 4, j // 4))
    p2 = pl.BlockSpec(blk, lambda b, j: (b, jnp.maximum(j % 4 - 1, 0), j // 4))
    c3 = pl.BlockSpec(blk, lambda b, j: (b, 0, j))
    (q1v, k1v, v1v), (q2v, k2v, v2v), (q3v, k3v, v3v) = views
    outs = pl.pallas_call(
        _att_body,
        grid=(B, 16),
        in_specs=[c1, p1, c1, p1, c1, c2, p2, c2, p2, c2, c3, c3, c3],
        out_specs=[c1, c2, c3, c1, c2, c3],
        out_shape=[jax.ShapeDtypeStruct(v[0].shape, BF16) for v in views]
        + [jax.ShapeDtypeStruct(v[0].shape, F32) for v in views],
        scratch_shapes=[pltpu.VMEM((N_GROUPS, HEADS, 2, CHUNK, CHUNK), F32)],
        compiler_params=pltpu.CompilerParams(
            dimension_semantics=("arbitrary", "arbitrary"), vmem_limit_bytes=VMEM_LIMIT),
        name="prompt_attention",
    )(q1v, k1v, k1v, v1v, v1v, q2v, k2v, k2v, v2v, v2v, q3v, k3v, v3v)
    return [o.reshape(B * S, GW) for o in outs]


def _merge_groups(o_refs, l_refs):
    ls = [r[...] for r in l_refs]
    m = jnp.maximum(jnp.maximum(ls[0], ls[1]), ls[2])
    es = [jnp.exp(l - m) for l in ls]
    num = es[0] * o_refs[0][...].astype(F32) + es[1] * o_refs[1][...].astype(F32) \
        + es[2] * o_refs[2][...].astype(F32)
    return num / (es[0] + es[1] + es[2])


def _rest_body(final, x_ref, o1, o2, o3, l1, l2, l3, ng, w, cw, cb, clg, clb, slg, slb, sw, sb,
               wpa, wpb, wpc, wout, fg, xo_ref, cs_ref, xp_s, acc_s, yb_s):
    j = pl.program_id(1)
    last = pl.num_programs(1) - 1
    x = x_ref[...]
    h = _rms(x, ng[...]).astype(BF16)

    def proj(c0, n):
        return jnp.dot(h, w[:, c0:c0 + n], preferred_element_type=F32)

    att = _merge_groups((o1, o2, o3), (l1, l2, l3))
    ya = (att * _silu(proj(C_GA, GW))).astype(BF16)
    pa = jnp.dot(ya, wpa[...], preferred_element_type=F32)
    acc_s[...] = _sigmoid(proj(C_GM, D_MODEL)) * pa

    @pl.when(j == 0)
    def _():
        xp_s[0:32, :] = jnp.zeros((32, GW), F32)

    xp_s[32:32 + TM, :] = proj(C_ZB, GW) * _sigmoid(proj(C_ZB + GW, GW))
    gb = _silu(proj(C_GB, GW))
    for r0 in range(0, TM, CHUNK):
        cols = []
        for c0 in range(0, GW, HEAD_DIM):
            a = jnp.zeros((CHUNK, HEAD_DIM), F32)
            for k in range(CONV_K):
                a = a + xp_s[r0 + k + 2:r0 + k + 2 + CHUNK, c0:c0 + HEAD_DIM] * cw[k:k + 1, c0:c0 + HEAD_DIM]
            cols.append(a)
        y = jnp.concatenate(cols, axis=1) + cb[...]
        y = _silu(_ln(y, clg[...], clb[...]))
        yb_s[r0:r0 + CHUNK, :] = (y * gb[r0:r0 + CHUNK, :]).astype(BF16)

    @pl.when(j == last)
    def _():
        cs_ref[0] = xp_s[TM + 2:TM + 32, :]

    xp_s[0:32, :] = xp_s[TM:TM + 32, :]
    pb = jnp.dot(yb_s[...], wpb[...], preferred_element_type=F32)
    acc_s[...] += _sigmoid(proj(C_GM + D_MODEL, D_MODEL)) * pb

    vn = _ln(proj(C_VC, GW), slg[...], slb[...]).astype(BF16)
    u = proj(C_U, GW)
    gc = _silu(proj(C_GC, GW))
    ii = lax.broadcasted_iota(jnp.int32, (CHUNK, CHUNK), 0)
    jj = lax.broadcasted_iota(jnp.int32, (CHUNK, CHUNK), 1)
    wms = [jnp.where(ii >= jj, sw[g], 0.0).astype(BF16) for g in range(4)]
    for r0 in range(0, TM, CHUNK):
        cols = []
        for g in range(4):
            gs = slice(g * HEAD_DIM, (g + 1) * HEAD_DIM)
            cols.append(jnp.dot(wms[g], vn[r0:r0 + CHUNK, gs], preferred_element_type=F32) + sb[g])
        s = jnp.concatenate(cols, axis=1)
        yb_s[r0:r0 + CHUNK, :] = (u[r0:r0 + CHUNK, :] * s * gc[r0:r0 + CHUNK, :]).astype(BF16)
    pc = jnp.dot(yb_s[...], wpc[...], preferred_element_type=F32)
    merged = acc_s[...] + _sigmoid(proj(C_GM + 2 * D_MODEL, D_MODEL)) * pc

    out = x + jnp.dot(merged.astype(BF16), wout[...], preferred_element_type=F32)
    if final:
        out = _rms(out, fg[...])
    xo_ref[...] = out


def _rest_call(final, x, o_l, p, B, S):
    nj = S // TM
    row = lambda n: pl.BlockSpec((TM, n), lambda b, j: (b * nj + j, 0))
    vec = _resident((1, GW))
    return pl.pallas_call(
        functools.partial(_rest_body, final),
        grid=(B, nj),
        in_specs=[row(D_MODEL)] + [row(GW)] * 6 + [
            _resident((1, D_MODEL)), _resident((D_MODEL, N_REST)),
            _resident((CONV_K, GW)), vec, vec, vec, vec, vec,
            _resident((4, CHUNK, CHUNK)), _resident((4, CHUNK, CHUNK)),
            _resident((GW, D_MODEL)), _resident((GW, D_MODEL)), _resident((GW, D_MODEL)),
            _resident((D_MODEL, D_MODEL)), _resident((1, D_MODEL))],
        out_specs=[row(D_MODEL), pl.BlockSpec((1, CONV_K - 1, GW), lambda b, j: (b, 0, 0))],
        out_shape=[jax.ShapeDtypeStruct((B * S, D_MODEL), F32),
                   jax.ShapeDtypeStruct((B, CONV_K - 1, GW), F32)],
        scratch_shapes=[pltpu.VMEM((TM + 32, GW), F32), pltpu.VMEM((TM, D_MODEL), F32),
                        pltpu.VMEM((TM, GW), BF16)],
        compiler_params=pltpu.CompilerParams(
            dimension_semantics=("arbitrary", "arbitrary"), vmem_limit_bytes=VMEM_LIMIT),
        name="prompt_rest",
    )(x, *o_l, p["norm_g"], p["w_rest"], p["conv_w"], p["conv_b"], p["conv_ln_g"], p["conv_ln_b"],
      p["sgu_ln_g"], p["sgu_ln_b"], p["sgu_w"], p["sgu_b_bc"], p["w_pa"], p["w_pb"], p["w_pc"],
      p["w_out"], p["final_g"])


def _sproj_body(x_ref, g_ref, w_ref, z_ref):
    h = _rms(x_ref[...], g_ref[...]).astype(BF16)
    z_ref[...] = jnp.dot(h, w_ref[...], preferred_element_type=F32)


def _sproj_call(x, norm_g, w, tn):
    n = w.shape[1]
    bd = x.shape[0]
    return pl.pallas_call(
        _sproj_body,
        grid=(n // tn,),
        in_specs=[pl.BlockSpec((bd, D_MODEL), lambda j: (0, 0)),
                  pl.BlockSpec((1, D_MODEL), lambda j: (0, 0)),
                  pl.BlockSpec((D_MODEL, tn), lambda j: (0, j))],
        out_specs=pl.BlockSpec((bd, tn), lambda j: (0, j)),
        out_shape=jax.ShapeDtypeStruct((bd, n), F32),
        compiler_params=pltpu.CompilerParams(dimension_semantics=("arbitrary",)),
        name="sample_proj",
    )(x, norm_g, w)


SBT = 8


def _satt_body(q_ref, k_ref, v_ref, ck1, cv1, ck2, cv2, ck3, cv3, att_ref):
    caches = ((ck1, cv1), (ck2, cv2), (ck3, cv3))
    a_col = lax.broadcasted_iota(jnp.int32, (CHUNK, 1), 0).astype(F32)

    def per_seq(bi, carry):
        outs = [[None] * HEADS for _ in range(N_GROUPS)]
        lses = [[None] * HEADS for _ in range(N_GROUPS)]
        for g, (_, dil) in enumerate(ATT_GROUPS):
            gs = slice(g * GW, (g + 1) * GW)
            kc = caches[g][0][0, bi]
            vc = caches[g][1][0, bi]
            qb = q_ref[pl.ds(bi, 1), gs]
            kn = k_ref[pl.ds(bi, 1), gs]
            vn = v_ref[pl.ds(bi, 1), gs]
            prod = kc * qb
            prod_n = kn * qb
            for h in range(HEADS):
                hs = slice(h * HEAD_DIM, (h + 1) * HEAD_DIM)
                bias = (-SLOPES[g][h] * dil) * (CHUNK - a_col)
                s = jnp.sum(prod[:, hs], axis=-1, keepdims=True) * SCALE + bias
                s_n = jnp.sum(prod_n[:, hs], axis=-1, keepdims=True) * SCALE
                m = jnp.maximum(jnp.max(s, axis=0, keepdims=True), s_n)
                p = jnp.exp(s - m)
                p_n = jnp.exp(s_n - m)
                l = jnp.sum(p, axis=0, keepdims=True) + p_n
                o = jnp.sum(p * vc[:, hs], axis=0, keepdims=True) + p_n * vn[:, hs]
                outs[g][h] = o / l
                lses[g][h] = m + jnp.log(l)
        cols = []
        for h in range(HEADS):
            m = jnp.maximum(jnp.maximum(lses[0][h], lses[1][h]), lses[2][h])
            es = [jnp.exp(lses[g][h] - m) for g in range(N_GROUPS)]
            num = es[0] * outs[0][h] + es[1] * outs[1][h] + es[2] * outs[2][h]
            cols.append(num / (es[0] + es[1] + es[2]))
        att_ref[pl.ds(bi, 1), :] = jnp.concatenate(cols, axis=1)
        return carry

    lax.fori_loop(0, SBT, per_seq, 0)


def _satt_call(zq, caches, layer, bd):
    in_specs = [pl.BlockSpec((SBT, ATT_QKV), lambda i, c=c: (i, c)) for c in range(3)]
    views = []
    for (ck, cv), (_, dil) in zip(caches, ATT_GROUPS):
        for a in (ck, cv):
            views.append(a.reshape(DEPTH, bd, CHUNK, dil * GW))
            in_specs.append(pl.BlockSpec((1, SBT, CHUNK, GW), lambda i: (layer, i, 0, 0)))
    return pl.pallas_call(
        _satt_body,
        grid=(bd // SBT,),
        in_specs=in_specs,
        out_specs=pl.BlockSpec((SBT, GW), lambda i: (i, 0)),
        out_shape=jax.ShapeDtypeStruct((bd, GW), F32),
        compiler_params=pltpu.CompilerParams(
            dimension_semantics=("arbitrary",), vmem_limit_bytes=VMEM_LIMIT),
        name="sample_attention",
    )(zq, zq, zq, *views)


SRT = 32


def _srest_body(final, x_ref, att_ref, z_ref, st_ref, cw, cb, clg, clb, slg, slb, sw0, sb0,
                wpa, wpb, wpc, wout, fg, xo_ref, st_out, vn_out):
    def zc(c0, n):
        return z_ref[:, c0:c0 + n]

    def mm(a, w_ref):
        return jnp.dot(a.astype(BF16), w_ref[...], preferred_element_type=F32)

    pa = mm(att_ref[...] * _silu(zc(C_GA, GW)), wpa)
    merged = _sigmoid(zc(C_GM, D_MODEL)) * pa

    a = zc(C_ZB, GW) * _sigmoid(zc(C_ZB + GW, GW))
    y = a * cw[CONV_K - 1:CONV_K, :] + cb[...]
    for k in range(CONV_K - 1):
        y = y + st_ref[:, k * GW:(k + 1) * GW] * cw[k:k + 1, :]
    st_out[:, 0:(CONV_K - 2) * GW] = st_ref[:, GW:(CONV_K - 1) * GW]
    st_out[:, (CONV_K - 2) * GW:] = a
    cy = _silu(_ln(y, clg[...], clb[...]))
    pb = mm(cy * _silu(zc(C_GB, GW)), wpb)
    merged = merged + _sigmoid(zc(C_GM + D_MODEL, D_MODEL)) * pb

    vn = _ln(zc(C_VC, GW), slg[...], slb[...])
    vn_out[...] = vn
    sy = zc(C_U, GW) * (sw0[...] * vn + sb0[...])
    pc = mm(sy * _silu(zc(C_GC, GW)), wpc)
    merged = merged + _sigmoid(zc(C_GM + 2 * D_MODEL, D_MODEL)) * pc

    out = x_ref[...] + mm(merged, wout)
    if final:
        out = _rms(out, fg[...])
    xo_ref[...] = out


def _srest_call(final, x, att, z_rest, state2d, p, bd):
    row = lambda n: pl.BlockSpec((SRT, n), lambda i: (i, 0))
    const = lambda shape: pl.BlockSpec(shape, lambda i: (0,) * len(shape))
    vec = const((1, GW))
    sw = (CONV_K - 1) * GW
    return pl.pallas_call(
        functools.partial(_srest_body, final),
        grid=(bd // SRT,),
        in_specs=[row(D_MODEL), row(GW), row(N_REST), row(sw),
                  const((CONV_K, GW)), vec, vec, vec, vec, vec, vec, vec,
                  const((GW, D_MODEL)), const((GW, D_MODEL)), const((GW, D_MODEL)),
                  const((D_MODEL, D_MODEL)), const((1, D_MODEL))],
        out_specs=[row(D_MODEL), row(sw), row(GW)],
        out_shape=[jax.ShapeDtypeStruct((bd, D_MODEL), F32),
                   jax.ShapeDtypeStruct((bd, sw), F32),
                   jax.ShapeDtypeStruct((bd, GW), F32)],
        compiler_params=pltpu.CompilerParams(
            dimension_semantics=("arbitrary",), vmem_limit_bytes=VMEM_LIMIT),
        name="sample_rest",
    )(x, att, z_rest, state2d, p["conv_w"], p["conv_b"], p["conv_ln_g"], p["conv_ln_b"],
      p["sgu_ln_g"], p["sgu_ln_b"], p["sgu_w0"], p["sgu_b0"], p["w_pa"], p["w_pb"], p["w_pc"],
      p["w_out"], p["final_g"])


def kernel(x_prompt, x_sample, cache_k_w128, cache_v_w128, cache_k_w512, cache_v_w512, cache_k_w2048, cache_v_w2048, state_conv, norm_g, w_in, conv_w, conv_b, conv_ln_g, conv_ln_b, sgu_ln_g, sgu_ln_b, sgu_w, sgu_b, w_pa, w_pb, w_pc, w_out, final_g):
    B, S, _ = x_prompt.shape
    bd = x_sample.shape[0]
    w_qkv_bf = w_in[:, :, :3 * ATT_QKV].astype(BF16)
    w_rest_bf = w_in[:, :, 3 * ATT_QKV:].astype(BF16)
    w_pa_bf, w_pb_bf, w_pc_bf, w_out_bf = (w.astype(BF16) for w in (w_pa, w_pb, w_pc, w_out))
    caches = ((cache_k_w128, cache_v_w128), (cache_k_w512, cache_v_w512), (cache_k_w2048, cache_v_w2048))
    state2d = state_conv.reshape(DEPTH, bd, (CONV_K - 1) * GW)

    xp = x_prompt.reshape(B * S, D_MODEL)
    xs = x_sample.reshape(bd, D_MODEL)
    kp = [[] for _ in range(N_GROUPS)]
    vp = [[] for _ in range(N_GROUPS)]
    ks = [[] for _ in range(N_GROUPS)]
    vs = [[] for _ in range(N_GROUPS)]
    conv_p, conv_s, sgu_v_s = [], [], []
    for l in range(DEPTH):
        final = l == DEPTH - 1
        p = dict(
            norm_g=norm_g[l][None], w_rest=w_rest_bf[l], conv_w=conv_w[l], conv_b=conv_b[l][None],
            conv_ln_g=conv_ln_g[l][None], conv_ln_b=conv_ln_b[l][None],
            sgu_ln_g=sgu_ln_g[l][None], sgu_ln_b=sgu_ln_b[l][None], sgu_w=sgu_w[l],
            sgu_b_bc=jnp.broadcast_to(sgu_b[l][:, :, None], (4, CHUNK, CHUNK)),
            sgu_w0=jnp.repeat(sgu_w[l][:, 0, 0], HEAD_DIM)[None],
            sgu_b0=jnp.repeat(sgu_b[l][:, 0], HEAD_DIM)[None],
            w_pa=w_pa_bf[l], w_pb=w_pb_bf[l], w_pc=w_pc_bf[l], w_out=w_out_bf[l], final_g=final_g[None])

        outs = _qkv_call(xp, p["norm_g"], w_qkv_bf[l], B, S)
        k1o, v1o, k2o, v2o, k3o, v3o = outs[9:]
        o_l = _att_call(outs[:9], B, S)
        xp, cbuf = _rest_call(final, xp, o_l, p, B, S)
        for g, (ko, vo) in enumerate(((k1o, v1o), (k2o, v2o), (k3o, v3o))):
            kp[g].append(ko.reshape(B, -1, HEADS, HEAD_DIM))
            vp[g].append(vo.reshape(B, -1, HEADS, HEAD_DIM))
        conv_p.append(cbuf)

        zq = _sproj_call(xs, p["norm_g"], w_qkv_bf[l], 512)
        zr = _sproj_call(xs, p["norm_g"], w_rest_bf[l], 512)
        att = _satt_call(zq, caches, l, bd)
        xs, st_new, vn = _srest_call(final, xs, att, zr, state2d[l], p, bd)
        for g in range(N_GROUPS):
            ks[g].append(zq[:, ATT_QKV + g * GW:ATT_QKV + (g + 1) * GW].reshape(bd, 1, HEADS, HEAD_DIM))
            vs[g].append(zq[:, 2 * ATT_QKV + g * GW:2 * ATT_QKV + (g + 1) * GW].reshape(bd, 1, HEADS, HEAD_DIM))
        conv_s.append(st_new.reshape(bd, CONV_K - 1, GW))
        sgu_v_s.append(vn.reshape(bd, 1, GW))

    st = jnp.stack
    return (xp.reshape(B, S, D_MODEL), xs.reshape(bd, 1, D_MODEL),
            st(kp[0]), st(vp[0]), st(kp[1]), st(vp[1]), st(kp[2]), st(vp[2]), st(conv_p),
            st(ks[0]), st(vs[0]), st(ks[1]), st(vs[1]), st(ks[2]), st(vs[2]), st(conv_s), st(sgu_v_s))
```
